```python
import math
import jax, jax.numpy as jnp
from jax import lax
import numpy as np

D_MODEL = 2048
BATCH = 2
SEQ = 16384
DEPTH = 1
DEC_BATCH = 32
DEC_SEQ = 64
PAST_LEN = 1024

CHUNK = 64
DA_HEADS = 8
DA_HD = 64
DA_VD = 2 * DA_HD
DA_W = DA_HEADS * 2 * DA_HD
CB_HEADS = 8
CB_HD = 128
CB_PREV = 8
CB_CLIP = 128
CB_W = CB_HEADS * CB_HD
X_HEADS = 4
X_HD = 128
X_W = X_HEADS * X_HD
N_MEM = 256
T5_BUCKETS = 32
T5_MAX_DIST = 128
D_FF = 5632
Q_BLOCK = 128
EPS = 1e-6
NEG_INF = -1e30
IN_COLS = 3 * DA_W + 3 * CB_W + 2 * D_MODEL
SPLITS = [DA_W, 2 * DA_W, 3 * DA_W, 3 * DA_W + CB_W, 3 * DA_W + 2 * CB_W, 3 * DA_W + 3 * CB_W]

kernel_name = 'hybrid_stream_diffattn_chunkband_step'


def rmsnorm(x, g):
    xf = x.astype(jnp.float32)
    y = xf * lax.rsqrt(jnp.mean(xf * xf, axis=-1, keepdims=True) + EPS)
    return (y * g.astype(jnp.float32)).astype(x.dtype)


def swiglu(x, w_gate, w_up, w_down):
    return (jax.nn.silu(x @ w_gate) * (x @ w_up)) @ w_down


def t5_bucket(rel):
    nb = T5_BUCKETS // 2
    max_exact = nb // 2
    base = jnp.where(rel > 0, nb, 0)
    n = jnp.abs(rel)
    nf = jnp.maximum(n, 1).astype(jnp.float32)
    large = max_exact + (jnp.log(nf / max_exact) / math.log(T5_MAX_DIST / max_exact)
                         * (nb - max_exact)).astype(jnp.int32)
    large = jnp.minimum(large, nb - 1)
    return base + jnp.where(n < max_exact, n, large)


def t5_bias_of(rel, table):
    return jnp.transpose(table[t5_bucket(rel)], (2, 0, 1)).astype(jnp.float32)


def band_bias_of(rel, table):
    return jnp.transpose(table[jnp.clip(rel, -CB_CLIP, CB_CLIP) + CB_CLIP], (2, 0, 1)).astype(jnp.float32)


def diff_attend(q, k, v, bias, mask, lam):
    s = jnp.einsum('bqhmd,bkhmd->bhmqk', q, k).astype(jnp.float32) * DA_HD ** -0.5 + bias[None, :, None]
    s = jnp.where(mask, s, NEG_INF)
    p = jax.nn.softmax(s, axis=-1)
    a = p[:, :, 0] - lam * p[:, :, 1]
    return jnp.einsum('bhqk,bkhd->bqhd', a.astype(v.dtype), v)


def diff_attn_prompt(q, k, v, t5_table, lam):
    B, S = q.shape[0], q.shape[1]
    k_pos = jnp.arange(S)

    def block(i):
        q0 = i * Q_BLOCK
        qb = lax.dynamic_slice_in_dim(q, q0, Q_BLOCK, axis=1)
        q_pos = q0 + jnp.arange(Q_BLOCK)
        rel = k_pos[None, :] - q_pos[:, None]
        mask = (k_pos[None, :] // CHUNK) <= (q_pos[:, None] // CHUNK)
        return diff_attend(qb, k, v, t5_bias_of(rel, t5_table), mask, lam)

    o = lax.map(block, jnp.arange(S // Q_BLOCK))
    return jnp.moveaxis(o, 0, 1).reshape(B, S, DA_HEADS, DA_VD)


def diff_attn_sample(q, k_new, v_new, k_cache, v_cache, t5_table, lam):
    B, T = q.shape[0], q.shape[1]
    P = k_cache.shape[1]
    k = jnp.concatenate([k_cache.reshape(B, P, DA_HEADS, 2, DA_HD), k_new], axis=1)
    v = jnp.concatenate([v_cache, v_new], axis=1)
    k_pos = jnp.arange(P + T)
    q_pos = P + jnp.arange(T)
    rel = k_pos[None, :] - q_pos[:, None]
    mask = (k_pos[None, :] // CHUNK) <= (q_pos[:, None] // CHUNK)
    return diff_attend(q, k, v, t5_bias_of(rel, t5_table), mask, lam)


def band_attn_prompt(q, k, v, rel_table):
    B, S = q.shape[0], q.shape[1]
    NC = S // CHUNK
    NB = CB_PREV + 1
    shp = (B, NC, CHUNK, CB_HEADS, CB_HD)
    pad = jnp.zeros((B, CB_PREV, CHUNK, CB_HEADS, CB_HD), k.dtype)
    kp = jnp.concatenate([pad, k.reshape(shp)], axis=1)
    vp = jnp.concatenate([pad, v.reshape(shp)], axis=1)
    idx = jnp.arange(NC)[:, None] + jnp.arange(NB)[None, :]
    kb = kp[:, idx].reshape(B, NC, NB * CHUNK, CB_HEADS, CB_HD)
    vb = vp[:, idx].reshape(B, NC, NB * CHUNK, CB_HEADS, CB_HD)
    kj = jnp.arange(NB * CHUNK)
    qi = jnp.arange(CHUNK)
    bias = band_bias_of(kj[None, :] - (CB_PREV * CHUNK + qi[:, None]), rel_table)
    valid = (jnp.arange(NC)[:, None] + kj[None, :] // CHUNK) >= CB_PREV
    s = jnp.einsum('bnqhd,bnkhd->bnhqk', q.reshape(shp), kb).astype(jnp.float32) * CB_HD ** -0.5 + bias[None, None]
    s = jnp.where(valid[None, :, None, None, :], s, NEG_INF)
    p = jax.nn.softmax(s, axis=-1)
    o = jnp.einsum('bnhqk,bnkhd->bnqhd', p.astype(v.dtype), vb)
    return o.reshape(B, S, CB_HEADS, CB_HD)


def band_attn_sample(q, k_new, v_new, k_buf, v_buf, rel_table, past_len):
    T = q.shape[1]
    R = k_buf.shape[1]
    k = jnp.concatenate([k_buf, k_new], axis=1)
    v = jnp.concatenate([v_buf, v_new], axis=1)
    k_pos = jnp.concatenate([past_len - R + jnp.arange(R), past_len + jnp.arange(T)])
    q_pos = past_len + jnp.arange(T)
    qc = q_pos[:, None] // CHUNK
    kc = k_pos[None, :] // CHUNK
    mask = (kc <= qc) & (kc >= qc - CB_PREV)
    bias = band_bias_of(k_pos[None, :] - q_pos[:, None], rel_table)
    s = jnp.einsum('bqhd,bkhd->bhqk', q, k).astype(jnp.float32) * CB_HD ** -0.5 + bias[None]
    s = jnp.where(mask, s, NEG_INF)
    p = jax.nn.softmax(s, axis=-1)
    o = jnp.einsum('bhqk,bkhd->bqhd', p.astype(v.dtype), v)
    return o, k[:, -R:], v[:, -R:]


def memory_kv(mem, g_mem, w_mem_kv):
    B, M = mem.shape[0], mem.shape[1]
    kv = rmsnorm(mem, g_mem) @ w_mem_kv
    mk, mv = jnp.split(kv, 2, axis=-1)
    return mk.reshape(B, M, X_HEADS, X_HD), mv.reshape(B, M, X_HEADS, X_HD)


def cross_attn(hn, mk, mv, w_q, w_o):
    B, T = hn.shape[0], hn.shape[1]
    q = (hn @ w_q).reshape(B, T, X_HEADS, X_HD)
    s = jnp.einsum('bqhd,bkhd->bhqk', q, mk).astype(jnp.float32) * X_HD ** -0.5
    p = jax.nn.softmax(s, axis=-1)
    o = jnp.einsum('bhqk,bkhd->bqhd', p.astype(mv.dtype), mv).reshape(B, T, X_W)
    return o @ w_o


def front(x, g1, w1g, w1u, w1d, g_mix, w_in, b_gate):
    B, T = x.shape[0], x.shape[1]
    h = x + 0.5 * swiglu(rmsnorm(x, g1), w1g, w1u, w1d)
    z = rmsnorm(h, g_mix) @ w_in
    dq, dk, dv, cq, ck, cv, gz = jnp.split(z, SPLITS, axis=-1)
    gates = jax.nn.sigmoid((gz + b_gate).astype(jnp.float32)).astype(x.dtype)
    return (h,
            dq.reshape(B, T, DA_HEADS, 2, DA_HD), dk.reshape(B, T, DA_HEADS, 2, DA_HD),
            dv.reshape(B, T, DA_HEADS, DA_VD),
            cq.reshape(B, T, CB_HEADS, CB_HD), ck.reshape(B, T, CB_HEADS, CB_HD),
            cv.reshape(B, T, CB_HEADS, CB_HD), gates)


def back(h, o_da, o_cb, gates, mem_k, mem_v, lam_init, da_g, w_br_da, w_br_cb, w_out,
         g_cross, w_cq, w_co, g2, w2g, w2u, w2d):
    B, T = h.shape[0], h.shape[1]
    o_da = rmsnorm(o_da, da_g) * (1.0 - lam_init)
    br_a = o_da.reshape(B, T, DA_HEADS * DA_VD) @ w_br_da
    br_b = o_cb.reshape(B, T, CB_W) @ w_br_cb
    g_a, g_b = jnp.split(gates, 2, axis=-1)
    h = h + (g_a * br_a + g_b * br_b) @ w_out
    h = h + cross_attn(rmsnorm(h, g_cross), mem_k, mem_v, w_cq, w_co)
    h = h + 0.5 * swiglu(rmsnorm(h, g2), w2g, w2u, w2d)
    return h


def setup_inputs(seed: int = 0) -> dict:
    key = jax.random.key(seed)
    ks = iter(jax.random.split(key, 48))
    f32 = jnp.float32

    def nrm(shape, scale=1.0):
        return jax.random.normal(next(ks), shape, f32) * scale

    def gain(shape):
        return 1.0 + nrm(shape, 0.02)

    L, D = DEPTH, D_MODEL
    cb_rows = min(CB_PREV * CHUNK, PAST_LEN)
    return {
        'x_prompt': nrm((BATCH, SEQ, D)),
        'x_sample': nrm((DEC_BATCH, DEC_SEQ, D)),
        'cache_da_k': nrm((L, DEC_BATCH, PAST_LEN, DA_HEADS, 2 * DA_HD)),
        'cache_da_v': nrm((L, DEC_BATCH, PAST_LEN, DA_HEADS, DA_VD)),
        'cache_cb_k': nrm((L, DEC_BATCH, cb_rows, CB_HEADS, CB_HD)),
        'cache_cb_v': nrm((L, DEC_BATCH, cb_rows, CB_HEADS, CB_HD)),
        'cache_mem_k': nrm((L, DEC_BATCH, N_MEM, X_HEADS, X_HD)),
        'cache_mem_v': nrm((L, DEC_BATCH, N_MEM, X_HEADS, X_HD)),
        'mem_prompt': nrm((BATCH, N_MEM, D)),
        't5_bias': nrm((T5_BUCKETS, DA_HEADS), 0.5),
        'g_ffn1': gain((L, D)),
        'w_ffn1_gate': nrm((L, D, D_FF), D ** -0.5),
        'w_ffn1_up': nrm((L, D, D_FF), D ** -0.5),
        'w_ffn1_down': nrm((L, D_FF, D), D_FF ** -0.5),
        'g_mix': gain((L, D)),
        'w_in': nrm((L, D, IN_COLS), D ** -0.5),
        'b_gate': nrm((L, 2 * D), 0.02),
        'da_lambda': nrm((L, 4, DA_HD), 0.1),
        'da_subln_g': gain((L, DA_VD)),
        'cb_rel_bias': nrm((L, 2 * CB_CLIP + 1, CB_HEADS), 0.5),
        'w_br_da': nrm((L, DA_HEADS * DA_VD, D), (DA_HEADS * DA_VD) ** -0.5),
        'w_br_cb': nrm((L, CB_W, D), CB_W ** -0.5),
        'w_out': nrm((L, D, D), D ** -0.5),
        'g_cross': gain((L, D)),
        'g_mem': gain((L, D)),
        'w_mem_kv': nrm((L, D, 2 * X_W), D ** -0.5),
        'w_cross_q': nrm((L, D, X_W), D ** -0.5),
        'w_cross_o': nrm((L, X_W, D), X_W ** -0.5),
        'g_ffn2': gain((L, D)),
        'w_ffn2_gate': nrm((L, D, D_FF), D ** -0.5),
        'w_ffn2_up': nrm((L, D, D_FF), D ** -0.5),
        'w_ffn2_down': nrm((L, D_FF, D), D_FF ** -0.5),
        'g_final': gain((D,)),
    }


def reference(x_prompt, x_sample, cache_da_k, cache_da_v, cache_cb_k, cache_cb_v, cache_mem_k, cache_mem_v,
              mem_prompt, t5_bias, g_ffn1, w_ffn1_gate, w_ffn1_up, w_ffn1_down, g_mix, w_in, b_gate,
              da_lambda, da_subln_g, cb_rel_bias, w_br_da, w_br_cb, w_out, g_cross, g_mem, w_mem_kv,
              w_cross_q, w_cross_o, g_ffn2, w_ffn2_gate, w_ffn2_up, w_ffn2_down, g_final):
    past_len = cache_da_k.shape[2]
    B, S = x_prompt.shape[0], x_prompt.shape[1]
    Bd, T = x_sample.shape[0], x_sample.shape[1]
    rows_p = min(CB_PREV * CHUNK, S)
    hp, hs = x_prompt, x_sample
    dkp, dvp, ckp, cvp, mkp, mvp = [], [], [], [], [], []
    dks, dvs, cks, cvs = [], [], [], []
    for l in range(DEPTH):
        lam_init = 0.8 - 0.6 * math.exp(-0.3 * l)
        lp = da_lambda[l].astype(jnp.float32)
        lam = jnp.exp(jnp.sum(lp[0] * lp[1])) - jnp.exp(jnp.sum(lp[2] * lp[3])) + lam_init
        fr = (g_ffn1[l], w_ffn1_gate[l], w_ffn1_up[l], w_ffn1_down[l], g_mix[l], w_in[l], b_gate[l])
        bk = (da_subln_g[l], w_br_da[l], w_br_cb[l], w_out[l], g_cross[l], w_cross_q[l], w_cross_o[l],
              g_ffn2[l], w_ffn2_gate[l], w_ffn2_up[l], w_ffn2_down[l])
        hp1, dq, dk, dv, cq, ck, cv, gp = front(hp, *fr)
        o_da = diff_attn_prompt(dq, dk, dv, t5_bias, lam)
        o_cb = band_attn_prompt(cq, ck, cv, cb_rel_bias[l])
        mk, mv = memory_kv(mem_prompt, g_mem[l], w_mem_kv[l])
        hp = back(hp1, o_da, o_cb, gp, mk, mv, lam_init, *bk)
        dkp.append(dk.reshape(B, S, DA_HEADS, 2 * DA_HD))
        dvp.append(dv)
        ckp.append(ck[:, S - rows_p:])
        cvp.append(cv[:, S - rows_p:])
        mkp.append(mk)
        mvp.append(mv)
        hs1, sq, sk, sv, tq, tk, tv, gs = front(hs, *fr)
        o_da_s = diff_attn_sample(sq, sk, sv, cache_da_k[l], cache_da_v[l], t5_bias, lam)
        o_cb_s, kbuf, vbuf = band_attn_sample(tq, tk, tv, cache_cb_k[l], cache_cb_v[l], cb_rel_bias[l], past_len)
        hs = back(hs1, o_da_s, o_cb_s, gs, cache_mem_k[l], cache_mem_v[l], lam_init, *bk)
        dks.append(sk.reshape(Bd, T, DA_HEADS, 2 * DA_HD))
        dvs.append(sv)
        cks.append(kbuf)
        cvs.append(vbuf)
    y_prompt = rmsnorm(hp, g_final)
    y_sample = rmsnorm(hs, g_final)
    return (y_prompt, y_sample,
            jnp.stack(dkp), jnp.stack(dvp), jnp.stack(ckp), jnp.stack(cvp), jnp.stack(mkp), jnp.stack(mvp),
            jnp.stack(dks), jnp.stack(dvs), jnp.stack(cks), jnp.stack(cvs))
```

```python
import functools
import math

import jax
import jax.numpy as jnp
import numpy as np
from jax import lax
from jax.experimental import pallas as pl
from jax.experimental.pallas import tpu as pltpu

CHUNK = 64
DA_HEADS = 8
DA_HD = 64
DA_VD = 2 * DA_HD
DA_W = DA_HEADS * 2 * DA_HD
CB_HEADS = 8
CB_HD = 128
CB_PREV = 8
CB_CLIP = 128
CB_W = CB_HEADS * CB_HD
X_HEADS = 4
X_HD = 128
X_W = X_HEADS * X_HD
T5_BUCKETS = 32
T5_MAX_DIST = 128
EPS = 1e-6
NEG_INF = -1e30
MASKED_BELOW = -1e29
NEVER_VALID = 3e38
HEAD_W = 128
STREAM_W = 1024
N_STREAMS = 6
IN_COLS = N_STREAMS * STREAM_W

VMEM_LIMIT_BYTES = 56 * 1024 * 1024

F32 = jnp.float32
BF16 = jnp.bfloat16


def _cparams(*sem):
    return pltpu.CompilerParams(dimension_semantics=sem, vmem_limit_bytes=VMEM_LIMIT_BYTES)


def _rms(x, g):
    return x * lax.rsqrt(jnp.mean(x * x, axis=-1, keepdims=True) + EPS) * g


def _dot(a, b):
    return jnp.dot(a, b, preferred_element_type=F32)


def _dot_nt(a, b):
    return lax.dot_general(a, b, (((1,), (1,)), ((), ())), preferred_element_type=F32)


def _tile(n, pref):
    t = min(n, pref)
    assert n % t == 0, (n, t)
    return t


def _ffn_kernel(*refs, final_norm):
    if final_norm:
        x_ref, g_ref, wg_ref, wu_ref, wd_ref, gf_ref, o_ref, xn_ref, acc_ref = refs
    else:
        x_ref, g_ref, wg_ref, wu_ref, wd_ref, o_ref, xn_ref, acc_ref = refs
    f = pl.program_id(1)

    @pl.when(f == 0)
    def _():
        xn_ref[...] = _rms(x_ref[...], g_ref[...]).astype(BF16)
        acc_ref[...] = jnp.zeros_like(acc_ref)

    xn = xn_ref[...]
    a = _dot(xn, wg_ref[...])
    u = _dot(xn, wu_ref[...])
    mid = (a * jax.nn.sigmoid(a) * u).astype(BF16)
    acc_ref[...] += _dot(mid, wd_ref[...])

    @pl.when(f == pl.num_programs(1) - 1)
    def _():
        h = x_ref[...] + 0.5 * acc_ref[...]
        if final_norm:
            h = _rms(h, gf_ref[...])
        o_ref[...] = h


def _ffn(x, g, wg, wu, wd, g_final=None):
    n, d = x.shape
    ff = wg.shape[1]
    tm = _tile(n, 512)
    tf = _tile(ff, 512)
    final_norm = g_final is not None
    in_specs = [
        pl.BlockSpec((tm, d), lambda i, f: (i, 0)),
        pl.BlockSpec((1, d), lambda i, f: (0, 0)),
        pl.BlockSpec((d, tf), lambda i, f: (0, f)),
        pl.BlockSpec((d, tf), lambda i, f: (0, f)),
        pl.BlockSpec((tf, d), lambda i, f: (f, 0)),
    ]
    args = [x, g.reshape(1, d), wg, wu, wd]
    if final_norm:
        in_specs.append(pl.BlockSpec((1, d), lambda i, f: (0, 0)))
        args.append(g_final.reshape(1, d))
    return pl.pallas_call(
        functools.partial(_ffn_kernel, final_norm=final_norm),
        name="ffn",
        grid=(n // tm, ff // tf),
        in_specs=in_specs,
        out_specs=pl.BlockSpec((tm, d), lambda i, f: (i, 0)),
        out_shape=jax.ShapeDtypeStruct((n, d), F32),
        scratch_shapes=[pltpu.VMEM((tm, d), BF16), pltpu.VMEM((tm, d), F32)],
        compiler_params=_cparams("parallel", "arbitrary"),
    )(*args)


def _mix_in_kernel(h_ref, g_ref, w_ref, b_ref, z_ref, dk_ref, dv_ref, ck_ref, cv_ref, hn_ref):
    j = pl.program_id(1)

    @pl.when(j == 0)
    def _():
        hn_ref[...] = _rms(h_ref[...], g_ref[...]).astype(BF16)

    zt = _dot(hn_ref[...], w_ref[...])

    @pl.when(j == 0)
    def _():
        z_ref[...] = (zt * (DA_HD ** -0.5)).astype(BF16)

    @pl.when(jnp.logical_and(j > 0, j < N_STREAMS))
    def _():
        z_ref[...] = zt.astype(BF16)

    @pl.when(j >= N_STREAMS)
    def _():
        z_ref[...] = jax.nn.sigmoid(zt + b_ref[...]).astype(BF16)

    for stream, ref in ((1, dk_ref), (2, dv_ref), (4, ck_ref), (5, cv_ref)):
        @pl.when(j == stream)
        def _(ref=ref):
            ref[...] = zt


def _mix_in(h, g, w_in, b_gate):
    n, d = h.shape
    cols = w_in.shape[1]
    tn = STREAM_W
    assert cols % tn == 0 and (cols - IN_COLS) == b_gate.shape[0]
    tm = _tile(n, 512)
    f32_out = jax.ShapeDtypeStruct((n, tn), F32)
    f32_spec = pl.BlockSpec((tm, tn), lambda i, j: (i, 0))
    return pl.pallas_call(
        _mix_in_kernel,
        name="mix_in",
        grid=(n // tm, cols // tn),
        in_specs=[
            pl.BlockSpec((tm, d), lambda i, j: (i, 0)),
            pl.BlockSpec((1, d), lambda i, j: (0, 0)),
            pl.BlockSpec((d, tn), lambda i, j: (0, j)),
            pl.BlockSpec((1, tn), lambda i, j: (0, jnp.maximum(j - N_STREAMS, 0))),
        ],
        out_specs=[pl.BlockSpec((tm, tn), lambda i, j: (i, j)), f32_spec, f32_spec, f32_spec, f32_spec],
        out_shape=[jax.ShapeDtypeStruct((n, cols), BF16), f32_out, f32_out, f32_out, f32_out],
        scratch_shapes=[pltpu.VMEM((tm, d), BF16)],
        compiler_params=_cparams("parallel", "arbitrary"),
    )(h, g.reshape(1, d), w_in, b_gate.reshape(1, -1))


def _mem_kv_kernel(m_ref, g_ref, w_ref, k_ref, v_ref):
    kv = _dot(_rms(m_ref[...], g_ref[...]).astype(BF16), w_ref[...])
    k_ref[...] = kv[:, :X_W]
    v_ref[...] = kv[:, X_W:]


def _mem_kv(mem, g, w):
    n, d = mem.shape
    tm = _tile(n, 256)
    out = jax.ShapeDtypeStruct((n, X_W), F32)
    return pl.pallas_call(
        _mem_kv_kernel,
        name="mem_kv",
        grid=(n // tm,),
        in_specs=[
            pl.BlockSpec((tm, d), lambda i: (i, 0)),
            pl.BlockSpec((1, d), lambda i: (0, 0)),
            pl.BlockSpec((d, 2 * X_W), lambda i: (0, 0)),
        ],
        out_specs=[pl.BlockSpec((tm, X_W), lambda i: (i, 0))] * 2,
        out_shape=[out, out],
        compiler_params=_cparams("parallel"),
    )(mem, g.reshape(1, d), w)


def _t5_bucket(rel):
    nb = T5_BUCKETS // 2
    max_exact = nb // 2
    base = jnp.where(rel > 0, nb, 0)
    n = jnp.abs(rel)
    nf = jnp.maximum(n, 1).astype(jnp.float32)
    large = max_exact + (jnp.log(nf / max_exact) / math.log(T5_MAX_DIST / max_exact)
                         * (nb - max_exact)).astype(jnp.int32)
    large = jnp.minimum(large, nb - 1)
    return base + jnp.where(n < max_exact, n, large)


def _t5_tile(table, rel, mask):
    b = jnp.transpose(table[_t5_bucket(jnp.asarray(rel, jnp.int32))], (2, 0, 1)).astype(F32)
    return jnp.where(jnp.asarray(mask)[None], b, NEG_INF)


def _band_tile(table, rel, mask):
    b = jnp.transpose(table[np.clip(rel, -CB_CLIP, CB_CLIP) + CB_CLIP], (2, 0, 1)).astype(F32)
    return jnp.where(jnp.asarray(mask)[None], b, NEG_INF)


def _split_maps(q):
    lane = lax.broadcasted_iota(jnp.int32, q.shape, 1)
    zero = jnp.zeros_like(q)
    return jnp.where(lane < DA_HD, q, zero), jnp.where(lane >= DA_HD, q, zero)


def _subln(o, g, lam_init):
    return _rms(o, g) * (1.0 - lam_init)


def _da_prompt_kernel(cfar_ref, lam_ref, q_ref, k_ref, v_ref, bias_ref, g_ref, o_ref,
                      m_ref, l_ref, acc_ref, *, tq, lam_init):
    h = pl.program_id(1)
    i = pl.program_id(2)
    qz = _split_maps(q_ref[0])
    cfar = cfar_ref[h]

    m_ref[...] = jnp.full_like(m_ref, NEG_INF)
    l_ref[...] = jnp.zeros_like(l_ref)
    acc_ref[...] = jnp.zeros_like(acc_ref)

    def update(mi, s, shift, vs):
        m_prev = m_ref[mi]
        m_new = jnp.maximum(m_prev, jnp.max(s, axis=-1, keepdims=True) + shift)
        p = jnp.exp(s - (m_new - shift))
        alpha = jnp.exp(m_prev - m_new)
        l_ref[mi] = alpha * l_ref[mi] + jnp.sum(p, axis=-1, keepdims=True)
        acc_ref[mi] = alpha * acc_ref[mi] + _dot(p.astype(BF16), vs)
        m_ref[mi] = m_new

    def far_body(j, carry):
        start = pl.multiple_of(j * tq, tq)
        ks = k_ref[0, pl.ds(start, tq), :]
        vs = v_ref[0, pl.ds(start, tq), :]
        for mi in range(2):
            update(mi, _dot_nt(qz[mi], ks), cfar, vs)
        return carry

    lax.fori_loop(0, jnp.maximum(i - 1, 0), far_body, 0)

    start = pl.multiple_of(jnp.maximum(i - 1, 0) * tq, tq)
    ks = k_ref[0, pl.ds(start, 2 * tq), :]
    vs = v_ref[0, pl.ds(start, 2 * tq), :]
    bias = bias_ref[0, 0]
    for mi in range(2):
        s = _dot_nt(qz[mi], ks)
        s = jnp.where(bias > MASKED_BELOW, s + bias, NEG_INF)
        update(mi, s, 0.0, vs)

    o = acc_ref[0] / l_ref[0] - lam_ref[0] * (acc_ref[1] / l_ref[1])
    o_ref[0] = _subln(o, g_ref[...], lam_init).astype(o_ref.dtype)


def _da_prompt(z, t5_table, lam, g, lam_init):
    b, s, _ = z.shape
    tq = _tile(s, 256)
    assert tq >= T5_MAX_DIST and tq % CHUNK == 0
    nq = s // tq
    r = np.arange(tq)[:, None]
    c = np.arange(2 * tq)[None, :]
    rel0, mask0 = c - r, (c // CHUNK) <= (r // CHUNK)
    rel1, mask1 = c - tq - r, ((c - tq) // CHUNK) <= (r // CHUNK)
    bias = jnp.stack([_t5_tile(t5_table, rel0, mask0), _t5_tile(t5_table, rel1, mask1)])
    cfar = t5_table[_t5_bucket(jnp.asarray(-(tq + 1), jnp.int32))].astype(F32)
    assert nq >= 2, "the near window spans two key tiles"
    kw = 2 * tq
    kv_rows = s
    smem = pl.BlockSpec(memory_space=pltpu.SMEM)
    return pl.pallas_call(
        functools.partial(_da_prompt_kernel, tq=tq, lam_init=lam_init),
        name="da_prompt",
        grid=(b, DA_HEADS, nq),
        in_specs=[
            smem, smem,
            pl.BlockSpec((1, tq, HEAD_W), lambda bi, h, i: (bi, i, h)),
            pl.BlockSpec((1, kv_rows, HEAD_W), lambda bi, h, i: (bi, 0, DA_HEADS + h)),
            pl.BlockSpec((1, kv_rows, HEAD_W), lambda bi, h, i: (bi, 0, 2 * DA_HEADS + h)),
            pl.BlockSpec((1, 1, tq, kw), lambda bi, h, i: (jnp.minimum(i, 1), h, 0, 0)),
            pl.BlockSpec((1, DA_VD), lambda bi, h, i: (0, 0)),
        ],
        out_specs=pl.BlockSpec((1, tq, HEAD_W), lambda bi, h, i: (bi, i, h)),
        out_shape=jax.ShapeDtypeStruct((b, s, DA_HEADS * DA_VD), BF16),
        scratch_shapes=[pltpu.VMEM((2, tq, 1), F32), pltpu.VMEM((2, tq, 1), F32),
                        pltpu.VMEM((2, tq, DA_VD), F32)],
        compiler_params=_cparams("parallel", "parallel", "arbitrary"),
    )(cfar, lam, z, z, z, bias, g.reshape(1, DA_VD))


def _softmax_pv(parts):
    m = functools.reduce(jnp.maximum, [jnp.max(s, axis=-1, keepdims=True) for s, _ in parts])
    ps = [jnp.exp(s - m) for s, _ in parts]
    l = functools.reduce(jnp.add, [jnp.sum(p, axis=-1, keepdims=True) for p in ps])
    o = functools.reduce(jnp.add, [_dot(p.astype(BF16), v) for p, (_, v) in zip(ps, parts)])
    return o / l


def _da_sample_kernel(lam_ref, q_ref, kn_ref, vn_ref, kc_ref, vc_ref, bias_ref, g_ref, o_ref, *, past, lam_init):
    for h in range(DA_HEADS):
        cols = slice(h * HEAD_W, (h + 1) * HEAD_W)
        qz = _split_maps(q_ref[0, :, cols])
        kc = kc_ref[0, :, cols].astype(BF16)
        vc = vc_ref[0, :, cols].astype(BF16)
        kn = kn_ref[0, :, cols]
        vn = vn_ref[0, :, cols]
        bias = bias_ref[h]
        bc, bn = bias[:, :past], bias[:, past:]
        outs = []
        for mi in range(2):
            sc = _dot_nt(qz[mi], kc)
            sn = _dot_nt(qz[mi], kn)
            sc = jnp.where(bc > MASKED_BELOW, sc + bc, NEG_INF)
            sn = jnp.where(bn > MASKED_BELOW, sn + bn, NEG_INF)
            outs.append(_softmax_pv([(sc, vc), (sn, vn)]))
        o = outs[0] - lam_ref[0] * outs[1]
        o_ref[0, :, cols] = _subln(o, g_ref[...], lam_init).astype(o_ref.dtype)


def _da_sample(z, cache_k, cache_v, t5_table, lam, g, lam_init):
    b, t, _ = z.shape
    past = cache_k.shape[1]
    k_pos = np.arange(past + t)[None, :]
    q_pos = (past + np.arange(t))[:, None]
    bias = _t5_tile(t5_table, k_pos - q_pos, (k_pos // CHUNK) <= (q_pos // CHUNK))
    zspec = lambda stream: pl.BlockSpec((1, t, STREAM_W), lambda bi: (bi, 0, stream))
    cspec = pl.BlockSpec((1, past, STREAM_W), lambda bi: (bi, 0, 0))
    return pl.pallas_call(
        functools.partial(_da_sample_kernel, past=past, lam_init=lam_init),
        name="da_sample",
        grid=(b,),
        in_specs=[
            pl.BlockSpec(memory_space=pltpu.SMEM),
            zspec(0), zspec(1), zspec(2), cspec, cspec,
            pl.BlockSpec((DA_HEADS, t, past + t), lambda bi: (0, 0, 0)),
            pl.BlockSpec((1, DA_VD), lambda bi: (0, 0)),
        ],
        out_specs=pl.BlockSpec((1, t, STREAM_W), lambda bi: (bi, 0, 0)),
        out_shape=jax.ShapeDtypeStruct((b, t, STREAM_W), BF16),
        compiler_params=_cparams("parallel"),
    )(lam, z, z, z, cache_k, cache_v, bias, g.reshape(1, DA_VD))


def _cb_prompt_kernel(q_ref, k0_ref, k1_ref, k2_ref, v0_ref, v1_ref, v2_ref, bias_ref, o_ref, *, tq):
    i = pl.program_id(2)
    q = q_ref[0]
    parts = []
    for t, (k_ref, v_ref) in enumerate(((k0_ref, v0_ref), (k1_ref, v1_ref), (k2_ref, v2_ref))):
        thresh = jnp.where(i >= 2 - t, MASKED_BELOW, NEVER_VALID)
        bias = bias_ref[0, :, t * tq:(t + 1) * tq]
        s = _dot_nt(q, k_ref[0]) * (CB_HD ** -0.5)
        parts.append((jnp.where(bias > thresh, s + bias, NEG_INF), v_ref[0]))
    o_ref[0] = _softmax_pv(parts).astype(o_ref.dtype)


def _cb_prompt(z, rel_table):
    b, s, _ = z.shape
    tq = _tile(s, 256)
    span = CB_PREV * CHUNK
    assert span == 2 * tq or s == tq, "band must be covered by two key tiles behind the query tile"
    r = np.arange(tq)[:, None]
    c = np.arange(3 * tq)[None, :] - 2 * tq
    qc, kc = r // CHUNK, c // CHUNK
    bias = _band_tile(rel_table, c - r, (kc <= qc) & (kc >= qc - CB_PREV))
    qs, ks, vs = 3 * CB_HEADS, 4 * CB_HEADS, 5 * CB_HEADS
    kv = lambda base, back: pl.BlockSpec(
        (1, tq, HEAD_W), lambda bi, h, i: (bi, jnp.maximum(i - back, 0), base + h))
    return pl.pallas_call(
        functools.partial(_cb_prompt_kernel, tq=tq),
        name="cb_prompt",
        grid=(b, CB_HEADS, s // tq),
        in_specs=[
            pl.BlockSpec((1, tq, HEAD_W), lambda bi, h, i: (bi, i, qs + h)),
            kv(ks, 2), kv(ks, 1), kv(ks, 0), kv(vs, 2), kv(vs, 1), kv(vs, 0),
            pl.BlockSpec((1, tq, 3 * tq), lambda bi, h, i: (h, 0, 0)),
        ],
        out_specs=pl.BlockSpec((1, tq, HEAD_W), lambda bi, h, i: (bi, i, h)),
        out_shape=jax.ShapeDtypeStruct((b, s, CB_W), BF16),
        compiler_params=_cparams("parallel", "parallel", "arbitrary"),
    )(z, z, z, z, z, z, z, bias)


def _cb_sample_kernel(q_ref, kn_ref, vn_ref, kn32_ref, vn32_ref, kc_ref, vc_ref, bias_ref,
                      o_ref, kout_ref, vout_ref, *, rows, t):
    for h in range(CB_HEADS):
        cols = slice(h * HEAD_W, (h + 1) * HEAD_W)
        q = q_ref[0, :, cols]
        bias = bias_ref[h]
        bc, bn = bias[:, :rows], bias[:, rows:]
        sc = _dot_nt(q, kc_ref[0, :, cols].astype(BF16)) * (CB_HD ** -0.5)
        sn = _dot_nt(q, kn_ref[0, :, cols]) * (CB_HD ** -0.5)
        sc = jnp.where(bc > MASKED_BELOW, sc + bc, NEG_INF)
        sn = jnp.where(bn > MASKED_BELOW, sn + bn, NEG_INF)
        o = _softmax_pv([(sc, vc_ref[0, :, cols].astype(BF16)), (sn, vn_ref[0, :, cols])])
        o_ref[0, :, cols] = o.astype(o_ref.dtype)
    kout_ref[0, :rows - t, :] = kc_ref[0, t:, :]
    kout_ref[0, rows - t:, :] = kn32_ref[0]
    vout_ref[0, :rows - t, :] = vc_ref[0, t:, :]
    vout_ref[0, rows - t:, :] = vn32_ref[0]


def _cb_sample(z, k_new32, v_new32, buf_k, buf_v, rel_table, past_len):
    b, t, _ = z.shape
    rows = buf_k.shape[1]
    assert t <= rows
    k_pos = np.concatenate([past_len - rows + np.arange(rows), past_len + np.arange(t)])[None, :]
    q_pos = (past_len + np.arange(t))[:, None]
    qc, kc = q_pos // CHUNK, k_pos // CHUNK
    bias = _band_tile(rel_table, k_pos - q_pos, (kc <= qc) & (kc >= qc - CB_PREV))
    zspec = lambda stream: pl.BlockSpec((1, t, STREAM_W), lambda bi: (bi, 0, stream))
    nspec = pl.BlockSpec((1, t, STREAM_W), lambda bi: (bi, 0, 0))
    cspec = pl.BlockSpec((1, rows, STREAM_W), lambda bi: (bi, 0, 0))
    buf_shape = jax.ShapeDtypeStruct((b, rows, STREAM_W), F32)
    return pl.pallas_call(
        functools.partial(_cb_sample_kernel, rows=rows, t=t),
        name="cb_sample",
        grid=(b,),
        in_specs=[zspec(3), zspec(4), zspec(5), nspec, nspec, cspec, cspec,
                  pl.BlockSpec((CB_HEADS, t, rows + t), lambda bi: (0, 0, 0))],
        out_specs=[nspec, cspec, cspec],
        out_shape=[jax.ShapeDtypeStruct((b, t, STREAM_W), BF16), buf_shape, buf_shape],
        compiler_params=_cparams("parallel"),
    )(z, z, z, k_new32, v_new32, buf_k, buf_v, bias)


def _back_kernel(h_ref, oda_ref, ocb_ref, ga_ref, gb_ref, mk_ref, mv_ref, wa_ref, wb_ref, wo_ref,
                 gx_ref, wq_ref, wco_ref, o_ref, att_ref, *, nb, tb):
    br_a = _dot(oda_ref[...], wa_ref[...])
    br_b = _dot(ocb_ref[...], wb_ref[...])
    mix = (ga_ref[...].astype(F32) * br_a + gb_ref[...].astype(F32) * br_b).astype(BF16)
    h2 = h_ref[...] + _dot(mix, wo_ref[...])
    q = _dot(_rms(h2, gx_ref[...]).astype(BF16), wq_ref[...]).astype(BF16)
    for bi in range(nb):
        rows = slice(bi * tb, (bi + 1) * tb)
        for hh in range(X_HEADS):
            cols = slice(hh * X_HD, (hh + 1) * X_HD)
            s = _dot_nt(q[rows, cols], mk_ref[bi, :, cols].astype(BF16)) * (X_HD ** -0.5)
            att_ref[rows, cols] = _softmax_pv([(s, mv_ref[bi, :, cols].astype(BF16))]).astype(BF16)
    o_ref[...] = h2 + _dot(att_ref[...], wco_ref[...])


def _back(h, o_da, o_cb, z, mem_k, mem_v, tokens_per_batch, w_br_da, w_br_cb, w_out, g_cross, w_cq, w_co):
    n, d = h.shape
    n_mem = mem_k.shape[1]
    tm = _tile(n, 256)
    if tokens_per_batch >= tm:
        assert tokens_per_batch % tm == 0
        nb, tb = 1, tm
        per = tokens_per_batch // tm
        mem_idx = lambda i: (i // per, 0, 0)
    else:
        assert tm % tokens_per_batch == 0
        nb, tb = tm // tokens_per_batch, tokens_per_batch
        mem_idx = lambda i: (i, 0, 0)
    assert IN_COLS % d == 0
    ga_blk = IN_COLS // d
    const = lambda shape: pl.BlockSpec(shape, lambda i: (0,) * len(shape), pipeline_mode=pl.Buffered(1))
    row = lambda w: pl.BlockSpec((tm, w), lambda i: (i, 0))
    return pl.pallas_call(
        functools.partial(_back_kernel, nb=nb, tb=tb),
        name="back",
        grid=(n // tm,),
        in_specs=[
            row(d), row(DA_W), row(CB_W),
            pl.BlockSpec((tm, d), lambda i: (i, ga_blk)),
            pl.BlockSpec((tm, d), lambda i: (i, ga_blk + 1)),
            pl.BlockSpec((nb, n_mem, X_W), mem_idx),
            pl.BlockSpec((nb, n_mem, X_W), mem_idx),
            const((DA_W, d)), const((CB_W, d)), const((d, d)),
            const((1, d)), const((d, X_W)), const((X_W, d)),
        ],
        out_specs=row(d),
        out_shape=jax.ShapeDtypeStruct((n, d), F32),
        scratch_shapes=[pltpu.VMEM((tm, X_W), BF16)],
        compiler_params=_cparams("parallel"),
    )(h, o_da, o_cb, z, z, mem_k, mem_v, w_br_da, w_br_cb, w_out, g_cross.reshape(1, d), w_cq, w_co)


def kernel(x_prompt, x_sample, cache_da_k, cache_da_v, cache_cb_k, cache_cb_v, cache_mem_k, cache_mem_v,
           mem_prompt, t5_bias, g_ffn1, w_ffn1_gate, w_ffn1_up, w_ffn1_down, g_mix, w_in, b_gate,
           da_lambda, da_subln_g, cb_rel_bias, w_br_da, w_br_cb, w_out, g_cross, g_mem, w_mem_kv,
           w_cross_q, w_cross_o, g_ffn2, w_ffn2_gate, w_ffn2_up, w_ffn2_down, g_final):
    depth = w_in.shape[0]
    past_len = cache_da_k.shape[2]
    b, s, d = x_prompt.shape
    bd, t, _ = x_sample.shape
    n_mem = mem_prompt.shape[1]
    rows_p = min(CB_PREV * CHUNK, s)
    rows_s = cache_cb_k.shape[2]

    hp = x_prompt.reshape(b * s, d)
    hs = x_sample.reshape(bd * t, d)
    outs = [[] for _ in range(10)]
    for l in range(depth):
        lam_init = 0.8 - 0.6 * math.exp(-0.3 * l)
        lp = da_lambda[l].astype(F32)
        lam = (jnp.exp(jnp.sum(lp[0] * lp[1])) - jnp.exp(jnp.sum(lp[2] * lp[3])) + lam_init).reshape(1)
        bf = lambda w: w[l].astype(BF16)
        w1 = (bf(w_ffn1_gate), bf(w_ffn1_up), bf(w_ffn1_down))
        w2 = (bf(w_ffn2_gate), bf(w_ffn2_up), bf(w_ffn2_down))
        w_in_l = bf(w_in)
        bk = (bf(w_br_da), bf(w_br_cb), bf(w_out), g_cross[l], bf(w_cross_q), bf(w_cross_o))
        gfin = g_final if l == depth - 1 else None

        h1 = _ffn(hp, g_ffn1[l], *w1)
        z, dk, dv, ck, cv = _mix_in(h1, g_mix[l], w_in_l, b_gate[l])
        z3 = z.reshape(b, s, -1)
        o_da = _da_prompt(z3, t5_bias, lam, da_subln_g[l], lam_init)
        o_cb = _cb_prompt(z3, cb_rel_bias[l])
        mk, mv = _mem_kv(mem_prompt.reshape(b * n_mem, d), g_mem[l], bf(w_mem_kv))
        h3 = _back(h1, o_da.reshape(b * s, DA_W), o_cb.reshape(b * s, CB_W), z,
                   mk.reshape(b, n_mem, X_W), mv.reshape(b, n_mem, X_W), s, *bk)
        hp = _ffn(h3, g_ffn2[l], *w2, g_final=gfin)
        outs[0].append(dk.reshape(b, s, DA_HEADS, 2 * DA_HD))
        outs[1].append(dv.reshape(b, s, DA_HEADS, DA_VD))
        outs[2].append(ck.reshape(b, s, CB_HEADS, CB_HD)[:, s - rows_p:])
        outs[3].append(cv.reshape(b, s, CB_HEADS, CB_HD)[:, s - rows_p:])
        outs[4].append(mk.reshape(b, n_mem, X_HEADS, X_HD))
        outs[5].append(mv.reshape(b, n_mem, X_HEADS, X_HD))

        h1 = _ffn(hs, g_ffn1[l], *w1)
        z, dk, dv, ck, cv = _mix_in(h1, g_mix[l], w_in_l, b_gate[l])
        z3 = z.reshape(bd, t, -1)
        o_da = _da_sample(z3, cache_da_k[l].reshape(bd, past_len, STREAM_W),
                          cache_da_v[l].reshape(bd, past_len, STREAM_W), t5_bias, lam, da_subln_g[l], lam_init)
        o_cb, kbuf, vbuf = _cb_sample(z3, ck.reshape(bd, t, STREAM_W), cv.reshape(bd, t, STREAM_W),
                                      cache_cb_k[l].reshape(bd, rows_s, STREAM_W),
                                      cache_cb_v[l].reshape(bd, rows_s, STREAM_W), cb_rel_bias[l], past_len)
        h3 = _back(h1, o_da.reshape(bd * t, DA_W), o_cb.reshape(bd * t, CB_W), z,
                   cache_mem_k[l].reshape(bd, n_mem, X_W), cache_mem_v[l].reshape(bd, n_mem, X_W), t, *bk)
        hs = _ffn(h3, g_ffn2[l], *w2, g_final=gfin)
        outs[6].append(dk.reshape(bd, t, DA_HEADS, 2 * DA_HD))
        outs[7].append(dv.reshape(bd, t, DA_HEADS, DA_VD))
        outs[8].append(kbuf.reshape(bd, rows_s, CB_HEADS, CB_HD))
        outs[9].append(vbuf.reshape(bd, rows_s, CB_HEADS, CB_HD))

    return (hp.reshape(b, s, d), hs.reshape(bd, t, d)) + tuple(jnp.stack(o) for o in outs)
```
